```python
import jax, jax.numpy as jnp
from jax import lax
import numpy as np

D_MODEL = 1024
BATCH = 2
SEQ = 8192
DEPTH = 2

CHUNK = 64
MIX_WIDTH = D_MODEL
CONV_WIDTH = MIX_WIDTH // 2
RET_WIDTH = MIX_WIDTH - CONV_WIDTH
RET_HEADS = 4
RET_HEAD_DIM = RET_WIDTH // RET_HEADS
CONV_KERNEL = 31
D_FF = 4 * D_MODEL
ROPE_BASE = 10000.0
EPS = 1e-6
IN_WIDTH = 2 * CONV_WIDTH + 4 * RET_WIDTH

kernel_name = "hybrid_conv_retention_encoder"


def rms_norm(x, g):
    xf = x.astype(jnp.float32)
    y = xf * lax.rsqrt(jnp.mean(xf * xf, axis=-1, keepdims=True) + EPS)
    return (y * g.astype(jnp.float32)).astype(x.dtype)


def layer_norm(x, g, b):
    xf = x.astype(jnp.float32)
    mu = jnp.mean(xf, axis=-1, keepdims=True)
    var = jnp.mean(jnp.square(xf - mu), axis=-1, keepdims=True)
    y = (xf - mu) * lax.rsqrt(var + EPS)
    return (y * g.astype(jnp.float32) + b.astype(jnp.float32)).astype(x.dtype)


def conv_group(u, w_dw, b_dw, ln_g, ln_b):
    a, gate = jnp.split(u, 2, axis=-1)
    h = a * jax.nn.sigmoid(gate)
    h = lax.conv_general_dilated(
        h, w_dw[:, None, :], window_strides=(1,),
        padding=[(CONV_KERNEL - 1, 0)],
        dimension_numbers=("NWC", "WIO", "NWC"),
        feature_group_count=CONV_WIDTH) + b_dw
    h = layer_norm(h, ln_g, ln_b)
    return jax.nn.silu(h)


def rotary(x, cos, sin):
    x1, x2 = jnp.split(x, 2, axis=-1)
    c = cos[None, :, None, :]
    s = sin[None, :, None, :]
    return jnp.concatenate([x1 * c - x2 * s, x1 * s + x2 * c], axis=-1)


def chunk_retention(q, k, v):
    b, s, h, d = q.shape
    n = s // CHUNK
    dt = q.dtype
    q = q.reshape(b, n, CHUNK, h, d)
    k = k.reshape(b, n, CHUNK, h, d)
    v = v.reshape(b, n, CHUNK, h, d)
    log_gamma = jnp.log(1.0 - jnp.exp2(-5.0 - jnp.arange(h, dtype=jnp.float32)))
    idx = jnp.arange(CHUNK, dtype=jnp.float32)
    dist = jnp.abs(idx[:, None] - idx[None, :])
    d_intra = jnp.exp(log_gamma[:, None, None] * dist[None]).astype(dt)
    w_key = jnp.exp(log_gamma[:, None] * (CHUNK - 1 - idx)[None]).astype(dt)
    w_qry = jnp.exp(log_gamma[:, None] * (idx + 1.0)[None]).astype(dt)
    g_chunk = jnp.exp(log_gamma * CHUNK).astype(dt)[None, :, None, None]

    scores = jnp.einsum("bnihd,bnjhd->bnhij", q, k) * d_intra
    o_intra = jnp.einsum("bnhij,bnjhe->bnihe", scores, v)

    kv = jnp.einsum("bnjhd,hj,bnjhe->nbhde", k, w_key, v)

    def step(state, kv_c):
        return state * g_chunk + kv_c, state

    _, s_prev = lax.scan(step, jnp.zeros((b, h, d, d), dt), kv)
    o_inter = jnp.einsum("bnihd,hi,nbhde->bnihe", q, w_qry, s_prev)
    return (o_intra + o_inter).reshape(b, s, h, d)


def retention_group(q, k, v, g, norm_g, cos, sin):
    b, s, _ = q.shape
    q = rotary(q.reshape(b, s, RET_HEADS, RET_HEAD_DIM), cos, sin)
    k = rotary(k.reshape(b, s, RET_HEADS, RET_HEAD_DIM), cos, sin) * (RET_HEAD_DIM ** -0.5)
    v = v.reshape(b, s, RET_HEADS, RET_HEAD_DIM)
    o = chunk_retention(q, k, v)
    o = rms_norm(o, norm_g.reshape(RET_HEADS, RET_HEAD_DIM))
    return o.reshape(b, s, RET_WIDTH) * jax.nn.silu(g)


def setup_inputs(seed: int = 0) -> dict:
    key = jax.random.key(seed)
    ks = jax.random.split(key, 16)
    f32 = jnp.float32
    nrm = lambda k, shape, scale: jax.random.normal(k, shape, f32) * scale
    return {
        "x": nrm(ks[0], (BATCH, SEQ, D_MODEL), 1.0),
        "norm1_g": 1.0 + nrm(ks[1], (DEPTH, D_MODEL), 0.02),
        "w_in": nrm(ks[2], (DEPTH, D_MODEL, IN_WIDTH), D_MODEL ** -0.5),
        "conv_w": nrm(ks[3], (DEPTH, CONV_KERNEL, CONV_WIDTH), CONV_KERNEL ** -0.5),
        "conv_b": nrm(ks[4], (DEPTH, CONV_WIDTH), 0.02),
        "conv_ln_g": 1.0 + nrm(ks[5], (DEPTH, CONV_WIDTH), 0.02),
        "conv_ln_b": nrm(ks[6], (DEPTH, CONV_WIDTH), 0.02),
        "ret_norm_g": 1.0 + nrm(ks[7], (DEPTH, RET_WIDTH), 0.02),
        "w_out": nrm(ks[8], (DEPTH, MIX_WIDTH, D_MODEL), MIX_WIDTH ** -0.5),
        "norm2_g": 1.0 + nrm(ks[9], (DEPTH, D_MODEL), 0.02),
        "w_ff1": nrm(ks[10], (DEPTH, D_MODEL, D_FF), D_MODEL ** -0.5),
        "w_ff2": nrm(ks[11], (DEPTH, D_FF, D_MODEL), D_FF ** -0.5),
        "final_g": 1.0 + nrm(ks[12], (D_MODEL,), 0.02),
    }


def reference(x, norm1_g, w_in, conv_w, conv_b, conv_ln_g, conv_ln_b, ret_norm_g,
              w_out, norm2_g, w_ff1, w_ff2, final_g):
    s = x.shape[1]
    pos = jnp.arange(s, dtype=jnp.float32)
    inv_freq = ROPE_BASE ** (-jnp.arange(0, RET_HEAD_DIM, 2, dtype=jnp.float32) / RET_HEAD_DIM)
    ang = pos[:, None] * inv_freq[None, :]
    cos = jnp.cos(ang).astype(x.dtype)
    sin = jnp.sin(ang).astype(x.dtype)
    split_at = [2 * CONV_WIDTH + i * RET_WIDTH for i in range(4)]

    for l in range(DEPTH):
        h = rms_norm(x, norm1_g[l])
        u = h @ w_in[l]
        u_conv, q, k, v, g = jnp.split(u, split_at, axis=-1)
        a_out = conv_group(u_conv, conv_w[l], conv_b[l], conv_ln_g[l], conv_ln_b[l])
        b_out = retention_group(q, k, v, g, ret_norm_g[l], cos, sin)
        x = x + jnp.concatenate([a_out, b_out], axis=-1) @ w_out[l]
        h = rms_norm(x, norm2_g[l])
        x = x + jnp.square(jax.nn.relu(h @ w_ff1[l])) @ w_ff2[l]
    return rms_norm(x, final_g)
```

```python
import functools

import jax
import jax.numpy as jnp
from jax import lax
from jax.experimental import pallas as pl
from jax.experimental.pallas import tpu as pltpu

D_MODEL = 1024
CHUNK = 64
CONV_WIDTH = 512
RET_WIDTH = 512
RET_HEADS = 4
RET_HEAD_DIM = 128
CONV_KERNEL = 31
D_FF = 4096
ROPE_BASE = 10000.0
EPS = 1e-6
IN_WIDTH = 2 * CONV_WIDTH + 4 * RET_WIDTH

LANES = 128
SUBLANES = 8
T_MIX = 512
L_RET = 256
T_FFN = 512
HALO = 32
CONV_ROWS = 64
VMEM_LIMIT_BYTES = 56 * 1024 * 1024

F32 = jnp.float32
BF16 = jnp.bfloat16


def _rms_scale(x):
    return lax.rsqrt(jnp.mean(x * x, axis=-1, keepdims=True) + EPS)


def _mixer_kernel(x_ref, g1_ref, win_ref, cw_ref, cb_ref, lng_ref, lnb_ref, rg_ref, wout_ref,
                  cos_ref, sin_ref, dmask_ref, wq_ref, wk_ref, gl_ref,
                  o_ref, hbuf_ref, conv_ref, ab_ref, state_ref):
    t = T_MIX

    @pl.when(pl.program_id(1) == 0)
    def _():
        hbuf_ref[0:HALO, :] = jnp.zeros((HALO, CONV_WIDTH), F32)
        state_ref[...] = jnp.zeros_like(state_ref)

    x = x_ref[0]
    h = (x * _rms_scale(x) * g1_ref[...]).astype(BF16)
    u = jnp.dot(h, win_ref[...], preferred_element_type=F32)

    hbuf_ref[HALO:HALO + t, :] = u[:, :CONV_WIDTH] * jax.nn.sigmoid(u[:, CONV_WIDTH:2 * CONV_WIDTH])
    base = HALO - (CONV_KERNEL - 1)
    for rb in range(t // CONV_ROWS):
        for cb in range(CONV_WIDTH // LANES):
            lanes = slice(cb * LANES, (cb + 1) * LANES)
            acc = None
            for j in range(CONV_KERNEL):
                tap = hbuf_ref[pl.ds(base + j + rb * CONV_ROWS, CONV_ROWS), lanes] * cw_ref[j:j + 1, lanes]
                acc = tap if acc is None else acc + tap
            conv_ref[rb * CONV_ROWS:(rb + 1) * CONV_ROWS, lanes] = acc
    hbuf_ref[0:HALO, :] = hbuf_ref[t:t + HALO, :]
    c = conv_ref[...] + cb_ref[...]
    mu = jnp.mean(c, axis=-1, keepdims=True)
    cc = c - mu
    var = jnp.mean(cc * cc, axis=-1, keepdims=True)
    a = cc * lax.rsqrt(var + EPS) * lng_ref[...] + lnb_ref[...]
    ab_ref[:, :CONV_WIDTH] = jax.nn.silu(a).astype(BF16)

    q0 = 2 * CONV_WIDTH
    cos = cos_ref[...]
    sin = sin_ref[...]
    for blk in range(t // L_RET):
        rows = slice(blk * L_RET, (blk + 1) * L_RET)
        for hd in range(RET_HEADS):
            lanes = slice(hd * LANES, (hd + 1) * LANES)
            q = u[rows, q0 + hd * LANES:q0 + (hd + 1) * LANES]
            k = u[rows, q0 + RET_WIDTH + hd * LANES:q0 + RET_WIDTH + (hd + 1) * LANES]
            v = u[rows, q0 + 2 * RET_WIDTH + hd * LANES:q0 + 2 * RET_WIDTH + (hd + 1) * LANES]
            g = u[rows, q0 + 3 * RET_WIDTH + hd * LANES:q0 + 3 * RET_WIDTH + (hd + 1) * LANES]
            q = q * cos[rows] + pltpu.roll(q, RET_HEAD_DIM // 2, 1) * sin[rows]
            k = (k * cos[rows] + pltpu.roll(k, RET_HEAD_DIM // 2, 1) * sin[rows]) * (RET_HEAD_DIM ** -0.5)
            vb = v.astype(BF16)
            s = lax.dot_general(q.astype(BF16), k.astype(BF16), (((1,), (1,)), ((), ())),
                                preferred_element_type=F32)
            p = (s * dmask_ref[hd]).astype(BF16)
            st = state_ref[hd]
            o = jnp.dot(p, vb, preferred_element_type=F32)
            o = o + jnp.dot((q * wq_ref[:, lanes]).astype(BF16), st.astype(BF16),
                            preferred_element_type=F32)
            kw = (k * wk_ref[:, lanes]).astype(BF16)
            state_ref[hd] = st * gl_ref[hd] + lax.dot_general(
                kw, vb, (((0,), (0,)), ((), ())), preferred_element_type=F32)
            on = o * _rms_scale(o) * rg_ref[:, lanes]
            ab_ref[rows, CONV_WIDTH + hd * LANES:CONV_WIDTH + (hd + 1) * LANES] = (
                on * jax.nn.silu(g)).astype(BF16)

    o_ref[0] = x + jnp.dot(ab_ref[...], wout_ref[...], preferred_element_type=F32)


def _ffn_kernel(x_ref, g2_ref, w1_ref, w2_ref, gf_ref, o_ref, *, final_norm):
    x = x_ref[0]
    h = (x * _rms_scale(x) * g2_ref[...]).astype(BF16)
    z = jnp.dot(h, w1_ref[...], preferred_element_type=F32)
    z = jnp.square(jnp.maximum(z, 0.0)).astype(BF16)
    y = x + jnp.dot(z, w2_ref[...], preferred_element_type=F32)
    if final_norm:
        y = y * _rms_scale(y) * gf_ref[...]
    o_ref[0] = y


def _const_spec(shape):
    return pl.BlockSpec(shape, lambda b, s: (0,) * len(shape), pipeline_mode=pl.Buffered(1))


def _mixer(x, g1, win, cw, cb, lng, lnb, rg, wout, cos2, sin2, dmask, wq, wk, gl):
    b, s, d = x.shape
    t = T_MIX
    row_spec = pl.BlockSpec((1, t, d), lambda bi, si: (bi, si, 0))
    tab_spec = pl.BlockSpec((t, LANES), lambda bi, si: (si, 0))
    return pl.pallas_call(
        _mixer_kernel,
        out_shape=jax.ShapeDtypeStruct(x.shape, F32),
        grid=(b, s // t),
        in_specs=[row_spec, _const_spec(g1.shape), _const_spec(win.shape), _const_spec(cw.shape),
                  _const_spec(cb.shape), _const_spec(lng.shape), _const_spec(lnb.shape),
                  _const_spec(rg.shape), _const_spec(wout.shape), tab_spec, tab_spec,
                  _const_spec(dmask.shape), _const_spec(wq.shape), _const_spec(wk.shape),
                  _const_spec(gl.shape)],
        out_specs=row_spec,
        scratch_shapes=[pltpu.VMEM((HALO + t + HALO, CONV_WIDTH), F32),
                        pltpu.VMEM((t, CONV_WIDTH), F32),
                        pltpu.VMEM((t, D_MODEL), BF16),
                        pltpu.VMEM((RET_HEADS, RET_HEAD_DIM, RET_HEAD_DIM), F32)],
        compiler_params=pltpu.CompilerParams(dimension_semantics=("arbitrary", "arbitrary"),
                                             vmem_limit_bytes=VMEM_LIMIT_BYTES),
        name="mixer",
    )(x, g1, win, cw, cb, lng, lnb, rg, wout, cos2, sin2, dmask, wq, wk, gl)


def _ffn(x, g2, w1, w2, gf, final_norm):
    b, s, d = x.shape
    t = T_FFN
    row_spec = pl.BlockSpec((1, t, d), lambda bi, si: (bi, si, 0))
    return pl.pallas_call(
        functools.partial(_ffn_kernel, final_norm=final_norm),
        out_shape=jax.ShapeDtypeStruct(x.shape, F32),
        grid=(b, s // t),
        in_specs=[row_spec, _const_spec(g2.shape), _const_spec(w1.shape), _const_spec(w2.shape),
                  _const_spec(gf.shape)],
        out_specs=row_spec,
        compiler_params=pltpu.CompilerParams(dimension_semantics=("arbitrary", "arbitrary"),
                                             vmem_limit_bytes=VMEM_LIMIT_BYTES),
        name="ffn",
    )(x, g2, w1, w2, gf)


def _tables(seq):
    pos = jnp.arange(seq, dtype=F32)
    inv_freq = ROPE_BASE ** (-jnp.arange(0, RET_HEAD_DIM, 2, dtype=F32) / RET_HEAD_DIM)
    ang = pos[:, None] * inv_freq[None, :]
    cos = jnp.cos(ang)
    sin = jnp.sin(ang)
    cos2 = jnp.concatenate([cos, cos], axis=-1)
    sin2 = jnp.concatenate([-sin, sin], axis=-1)

    log_gamma = jnp.log(1.0 - jnp.exp2(-5.0 - jnp.arange(RET_HEADS, dtype=F32)))
    idx = jnp.arange(L_RET, dtype=F32)
    chunk = jnp.floor(idx / CHUNK)
    diff = idx[:, None] - idx[None, :]
    same = chunk[:, None] == chunk[None, :]
    earlier = chunk[None, :] < chunk[:, None]
    dist = jnp.where(same, jnp.abs(diff), diff)
    dmask = jnp.where((same | earlier)[None], jnp.exp(log_gamma[:, None, None] * dist[None]), 0.0)
    wq = jnp.exp(log_gamma[:, None] * (idx + 1.0)[None])
    wk = jnp.exp(log_gamma[:, None] * (L_RET - 1 - idx)[None])
    wq = jnp.repeat(wq.T, RET_HEAD_DIM, axis=1)
    wk = jnp.repeat(wk.T, RET_HEAD_DIM, axis=1)
    gl = jnp.broadcast_to(jnp.exp(log_gamma * L_RET)[:, None, None], (RET_HEADS, 1, RET_HEAD_DIM))
    return cos2, sin2, dmask.astype(F32), wq, wk, gl


def kernel(x, norm1_g, w_in, conv_w, conv_b, conv_ln_g, conv_ln_b, ret_norm_g, w_out, norm2_g,
           w_ff1, w_ff2, final_g):
    b, s, d = x.shape
    depth = w_in.shape[0]
    assert (d, s % T_MIX, s % T_FFN, T_MIX % L_RET, L_RET % CHUNK) == (D_MODEL, 0, 0, 0, 0)
    cos2, sin2, dmask, wq, wk, gl = _tables(s)
    row = lambda v: v.reshape(1, -1)
    for l in range(depth):
        x = _mixer(x, row(norm1_g[l]), w_in[l].astype(BF16), conv_w[l], row(conv_b[l]),
                   row(conv_ln_g[l]), row(conv_ln_b[l]), row(ret_norm_g[l]), w_out[l].astype(BF16),
                   cos2, sin2, dmask, wq, wk, gl)
        x = _ffn(x, row(norm2_g[l]), w_ff1[l].astype(BF16), w_ff2[l].astype(BF16), row(final_g),
                 final_norm=(l == depth - 1))
    return x
```

```python
import functools

import jax
import jax.numpy as jnp
import numpy as np
from jax import lax
from jax.experimental import pallas as pl
from jax.experimental.pallas import tpu as pltpu

D_MODEL = 1024
CHUNK = 64
CONV_WIDTH = 512
RET_WIDTH = 512
RET_HEADS = 4
RET_HEAD_DIM = 128
CONV_KERNEL = 31
D_FF = 4096
ROPE_BASE = 10000.0
EPS = 1e-6
IN_WIDTH = 2 * CONV_WIDTH + 4 * RET_WIDTH

LANES = 128
SUBLANES = 8
T_MIX = 512
L_RET = 256
T_FFN = 512
HALO = 32
CONV_ROWS = 64
VMEM_LIMIT_BYTES = 56 * 1024 * 1024
assert HALO % SUBLANES == 0 and HALO >= SUBLANES * ((CONV_KERNEL - 1) // SUBLANES + 1)

F32 = jnp.float32
BF16 = jnp.bfloat16


def _rms_scale(x):
    return lax.rsqrt(jnp.mean(x * x, axis=-1, keepdims=True) + EPS)


def _mixer_kernel(x_ref, g1_ref, win_ref, cw_ref, cb_ref, lng_ref, lnb_ref, rg_ref, wout_ref,
                  cos_ref, sin_ref, dmask_ref, wq_ref, wk_ref, gl_ref,
                  o_ref, hbuf_ref, conv_ref, ab_ref, state_ref):
    t = T_MIX

    @pl.when(pl.program_id(1) == 0)
    def _():
        hbuf_ref[0:HALO, :] = jnp.zeros((HALO, CONV_WIDTH), F32)
        state_ref[...] = jnp.zeros_like(state_ref)

    x = x_ref[0]
    h = (x * _rms_scale(x) * g1_ref[...]).astype(BF16)
    u = jnp.dot(h, win_ref[...], preferred_element_type=F32)

    hbuf_ref[HALO:HALO + t, :] = u[:, :CONV_WIDTH] * jax.nn.sigmoid(u[:, CONV_WIDTH:2 * CONV_WIDTH])
    for rb in range(t // CONV_ROWS):
        for cb in range(CONV_WIDTH // LANES):
            lanes = slice(cb * LANES, (cb + 1) * LANES)
            acc = None
            for b in range(SUBLANES):
                part = None
                for a in range((CONV_KERNEL - 1 - b) // SUBLANES + 1):
                    j = CONV_KERNEL - 1 - (SUBLANES * a + b)
                    start = HALO + rb * CONV_ROWS - SUBLANES * (a + 1)
                    tap = hbuf_ref[start:start + CONV_ROWS + SUBLANES, lanes] * cw_ref[j:j + 1, lanes]
                    part = tap if part is None else part + tap
                part = part[SUBLANES - b:SUBLANES - b + CONV_ROWS]
                acc = part if acc is None else acc + part
            conv_ref[rb * CONV_ROWS:(rb + 1) * CONV_ROWS, lanes] = acc
    hbuf_ref[0:HALO, :] = hbuf_ref[t:t + HALO, :]
    c = conv_ref[...] + cb_ref[...]
    mu = jnp.mean(c, axis=-1, keepdims=True)
    cc = c - mu
    var = jnp.mean(cc * cc, axis=-1, keepdims=True)
    a = cc * lax.rsqrt(var + EPS) * lng_ref[...] + lnb_ref[...]
    ab_ref[:, :CONV_WIDTH] = jax.nn.silu(a).astype(BF16)

    q0 = 2 * CONV_WIDTH
    cos = cos_ref[...]
    sin = sin_ref[...]
    for blk in range(t // L_RET):
        rows = slice(blk * L_RET, (blk + 1) * L_RET)
        for hd in range(RET_HEADS):
            lanes = slice(hd * LANES, (hd + 1) * LANES)
            q = u[rows, q0 + hd * LANES:q0 + (hd + 1) * LANES]
            k = u[rows, q0 + RET_WIDTH + hd * LANES:q0 + RET_WIDTH + (hd + 1) * LANES]
            v = u[rows, q0 + 2 * RET_WIDTH + hd * LANES:q0 + 2 * RET_WIDTH + (hd + 1) * LANES]
            g = u[rows, q0 + 3 * RET_WIDTH + hd * LANES:q0 + 3 * RET_WIDTH + (hd + 1) * LANES]
            q = q * cos[rows] + pltpu.roll(q, RET_HEAD_DIM // 2, 1) * sin[rows]
            k = (k * cos[rows] + pltpu.roll(k, RET_HEAD_DIM // 2, 1) * sin[rows]) * (RET_HEAD_DIM ** -0.5)
            vb = v.astype(BF16)
            s = lax.dot_general(q.astype(BF16), k.astype(BF16), (((1,), (1,)), ((), ())),
                                preferred_element_type=F32)
            p = (s * dmask_ref[hd]).astype(BF16)
            st = state_ref[hd]
            o = jnp.dot(p, vb, preferred_element_type=F32)
            o = o + jnp.dot((q * wq_ref[:, lanes]).astype(BF16), st.astype(BF16),
                            preferred_element_type=F32)
            kw = (k * wk_ref[:, lanes]).astype(BF16)
            state_ref[hd] = st * gl_ref[hd] + lax.dot_general(
                kw, vb, (((0,), (0,)), ((), ())), preferred_element_type=F32)
            on = o * _rms_scale(o) * rg_ref[:, lanes]
            ab_ref[rows, CONV_WIDTH + hd * LANES:CONV_WIDTH + (hd + 1) * LANES] = (
                on * jax.nn.silu(g)).astype(BF16)

    o_ref[0] = x + jnp.dot(ab_ref[...], wout_ref[...], preferred_element_type=F32)


def _ffn_kernel(x_ref, g2_ref, w1_ref, w2_ref, gf_ref, o_ref, *, final_norm):
    x = x_ref[0]
    h = (x * _rms_scale(x) * g2_ref[...]).astype(BF16)
    z = jnp.dot(h, w1_ref[...], preferred_element_type=F32)
    z = jnp.square(jnp.maximum(z, 0.0)).astype(BF16)
    y = x + jnp.dot(z, w2_ref[...], preferred_element_type=F32)
    if final_norm:
        y = y * _rms_scale(y) * gf_ref[...]
    o_ref[0] = y


def _const_spec(shape):
    return pl.BlockSpec(shape, lambda b, s: (0,) * len(shape), pipeline_mode=pl.Buffered(1))


def _layer_spec(arr, l):
    return pl.BlockSpec((None,) + arr.shape[1:], lambda b, s: (l, 0, 0), pipeline_mode=pl.Buffered(1))


def _mixer(l, x, g1, win, cw, cb, lng, lnb, rg, wout, cos2, sin2, dmask, wq, wk, gl):
    b, s, d = x.shape
    t = T_MIX
    row_spec = pl.BlockSpec((1, t, d), lambda bi, si: (bi, si, 0))
    tab_spec = pl.BlockSpec((t, LANES), lambda bi, si: (si, 0))
    params = (g1, win, cw, cb, lng, lnb, rg, wout)
    tables = (dmask, wq, wk, gl)
    return pl.pallas_call(
        _mixer_kernel,
        out_shape=jax.ShapeDtypeStruct(x.shape, F32),
        grid=(b, s // t),
        in_specs=[row_spec, *[_layer_spec(p, l) for p in params], tab_spec, tab_spec,
                  *[_const_spec(v.shape) for v in tables]],
        out_specs=row_spec,
        scratch_shapes=[pltpu.VMEM((HALO + t + HALO, CONV_WIDTH), F32),
                        pltpu.VMEM((t, CONV_WIDTH), F32),
                        pltpu.VMEM((t, D_MODEL), BF16),
                        pltpu.VMEM((RET_HEADS, RET_HEAD_DIM, RET_HEAD_DIM), F32)],
        compiler_params=pltpu.CompilerParams(dimension_semantics=("arbitrary", "arbitrary"),
                                             vmem_limit_bytes=VMEM_LIMIT_BYTES),
        name="mixer",
    )(x, *params, cos2, sin2, *tables)


def _ffn(l, x, g2, w1, w2, gf, final_norm):
    b, s, d = x.shape
    t = T_FFN
    row_spec = pl.BlockSpec((1, t, d), lambda bi, si: (bi, si, 0))
    return pl.pallas_call(
        functools.partial(_ffn_kernel, final_norm=final_norm),
        out_shape=jax.ShapeDtypeStruct(x.shape, F32),
        grid=(b, s // t),
        in_specs=[row_spec, _layer_spec(g2, l), _layer_spec(w1, l), _layer_spec(w2, l),
                  _const_spec(gf.shape)],
        out_specs=row_spec,
        compiler_params=pltpu.CompilerParams(dimension_semantics=("arbitrary", "arbitrary"),
                                             vmem_limit_bytes=VMEM_LIMIT_BYTES),
        name="ffn",
    )(x, g2, w1, w2, gf)


def _tables(seq):
    f32 = np.float32
    pos = np.arange(seq, dtype=f32)
    inv_freq = (f32(ROPE_BASE) ** (-np.arange(0, RET_HEAD_DIM, 2, dtype=f32) / f32(RET_HEAD_DIM))).astype(f32)
    ang = pos[:, None] * inv_freq[None, :]
    cos = np.cos(ang)
    sin = np.sin(ang)
    cos2 = np.concatenate([cos, cos], axis=-1)
    sin2 = np.concatenate([-sin, sin], axis=-1)

    log_gamma = np.log(f32(1.0) - np.exp2(f32(-5.0) - np.arange(RET_HEADS, dtype=f32)))
    idx = np.arange(L_RET, dtype=f32)
    chunk = np.floor(idx / CHUNK)
    diff = idx[:, None] - idx[None, :]
    same = chunk[:, None] == chunk[None, :]
    earlier = chunk[None, :] < chunk[:, None]
    dist = np.where(same, np.abs(diff), diff)
    dmask = np.where((same | earlier)[None], np.exp(log_gamma[:, None, None] * dist[None]), f32(0.0))
    wq = np.exp(log_gamma[:, None] * (idx + f32(1.0))[None])
    wk = np.exp(log_gamma[:, None] * (f32(L_RET - 1) - idx)[None])
    wq = np.repeat(wq.T, RET_HEAD_DIM, axis=1)
    wk = np.repeat(wk.T, RET_HEAD_DIM, axis=1)
    gl = np.broadcast_to(np.exp(log_gamma * f32(L_RET))[:, None, None], (RET_HEADS, 1, RET_HEAD_DIM))
    return tuple(jnp.asarray(np.ascontiguousarray(v), dtype=F32) for v in (cos2, sin2, dmask, wq, wk, gl))


def kernel(x, norm1_g, w_in, conv_w, conv_b, conv_ln_g, conv_ln_b, ret_norm_g, w_out, norm2_g,
           w_ff1, w_ff2, final_g):
    b, s, d = x.shape
    depth = w_in.shape[0]
    assert (d, s % T_MIX, s % T_FFN, T_MIX % L_RET, L_RET % CHUNK) == (D_MODEL, 0, 0, 0, 0)
    cos2, sin2, dmask, wq, wk, gl = _tables(s)
    rows = lambda v: v.reshape(depth, 1, -1)
    g1, cb, lng, lnb, rg, g2 = map(rows, (norm1_g, conv_b, conv_ln_g, conv_ln_b, ret_norm_g, norm2_g))
    win, wout, w1, w2 = (w.astype(BF16) for w in (w_in, w_out, w_ff1, w_ff2))
    gf = final_g.reshape(1, -1)
    for l in range(depth):
        x = _mixer(l, x, g1, win, conv_w, cb, lng, lnb, rg, wout, cos2, sin2, dmask, wq, wk, gl)
        x = _ffn(l, x, g2, w1, w2, gf, final_norm=(l == depth - 1))
    return x
```
